```python
import math
import jax, jax.numpy as jnp
from jax import lax
import numpy as np

D_MODEL = 1024
BATCH = 2
SEQ = 16384
DEPTH = 2
DEC_BATCH = 32
DEC_SEQ = 16
PAST_LEN = 2048

CHUNK = 64
N_MIXERS = 2
N_SSM_LAYERS = (DEPTH + 1) // 2
N_ATTN_LAYERS = DEPTH // 2
SSM_GROUP = 16
SSM_GROUPS = D_MODEL // SSM_GROUP
SSM_STATE = 64
DT_MIN = 1e-3
DT_MAX = 1e-1
HEAD_DIM = 64
N_HEADS = D_MODEL // HEAD_DIM
N_KV_HEADS = 2
GQA_GROUP = N_HEADS // N_KV_HEADS
WINDOW = 128
WIN_CHUNKS = WINDOW // CHUNK
ROPE_THETA = 10000.0
QKV_DIM = (N_HEADS + 2 * N_KV_HEADS) * HEAD_DIM
N_EXPERTS = 256
TOP_K = 8
N_EXPERT_GROUPS = 8
TOPK_GROUPS = 4
EXPERT_FF = 256
SHARED_FF = 256
ROUTED_SCALE = 2.5
MOE_BLOCK = 128
RMS_EPS = 1e-6
NEG_INF = -1e30

kernel_name = 'hybrid_s5_swa_moe_stream_step'


def rmsnorm(x, g):
    x32 = x.astype(jnp.float32)
    y = x32 * lax.rsqrt(jnp.mean(x32 * x32, axis=-1, keepdims=True) + RMS_EPS)
    return (y * g.astype(jnp.float32)).astype(x.dtype)


def modulation(c, w, b):
    m = (jax.nn.silu(c) @ w + b).reshape(c.shape[0], 6, 1, D_MODEL)
    return [m[:, n] for n in range(6)]


def rope(x, pos):
    half = HEAD_DIM // 2
    inv = ROPE_THETA ** (-jnp.arange(half, dtype=jnp.float32) / half)
    ang = pos.astype(jnp.float32)[:, None] * inv[None, :]
    cos, sin = jnp.cos(ang)[:, None, :], jnp.sin(ang)[:, None, :]
    x32 = x.astype(jnp.float32)
    x1, x2 = x32[..., :half], x32[..., half:]
    return jnp.concatenate([x1 * cos - x2 * sin, x2 * cos + x1 * sin], axis=-1).astype(x.dtype)


def s5_mixer(u, h0_re, h0_im, a_re, a_im, log_dt, b_re, b_im, c_re, c_im, d_skip, w_glu):
    f32 = jnp.float32
    bs, length, _ = u.shape
    blk = CHUNK if length % CHUNK == 0 else length
    n_blk = length // blk
    dt = jnp.exp(log_dt.astype(f32))[:, None]
    ar, ai = a_re.astype(f32), a_im.astype(f32)
    mag = jnp.exp(dt * ar)
    abar_re, abar_im = mag * jnp.cos(dt * ai), mag * jnp.sin(dt * ai)
    den = ar * ar + ai * ai
    coef_re = ((abar_re - 1.0) * ar + abar_im * ai) / den
    coef_im = (abar_im * ar - (abar_re - 1.0) * ai) / den
    br, bi = b_re.astype(f32), b_im.astype(f32)
    bbar_re = coef_re[..., None] * br - coef_im[..., None] * bi
    bbar_im = coef_re[..., None] * bi + coef_im[..., None] * br
    cr, ci = c_re.astype(f32), c_im.astype(f32)
    a_re_blk = jnp.broadcast_to(abar_re, (bs, blk, SSM_GROUPS, SSM_STATE))
    a_im_blk = jnp.broadcast_to(abar_im, (bs, blk, SSM_GROUPS, SSM_STATE))
    u32 = u.astype(f32)
    u_blocks = u32.reshape(bs, n_blk, blk, SSM_GROUPS, SSM_GROUP).transpose(1, 0, 2, 3, 4)

    def combine(left, right):
        a1r, a1i, x1r, x1i = left
        a2r, a2i, x2r, x2i = right
        return (a2r * a1r - a2i * a1i, a2r * a1i + a2i * a1r,
                a2r * x1r - a2i * x1i + x2r, a2r * x1i + a2i * x1r + x2i)

    def block_step(carry, ub):
        hr, hi = carry
        xr = jnp.einsum('blgc,gpc->blgp', ub, bbar_re)
        xi = jnp.einsum('blgc,gpc->blgp', ub, bbar_im)
        xr = xr.at[:, 0].add(abar_re * hr - abar_im * hi)
        xi = xi.at[:, 0].add(abar_re * hi + abar_im * hr)
        _, _, sr, si = lax.associative_scan(combine, (a_re_blk, a_im_blk, xr, xi), axis=1)
        y = jnp.einsum('blgp,gcp->blgc', sr, cr) - jnp.einsum('blgp,gcp->blgc', si, ci)
        return (sr[:, -1], si[:, -1]), y

    (h_re, h_im), ys = lax.scan(block_step, (h0_re.astype(f32), h0_im.astype(f32)), u_blocks)
    y = ys.transpose(1, 0, 2, 3, 4).reshape(bs, length, D_MODEL) + d_skip.astype(f32) * u32
    z = jax.nn.gelu(y)
    val, gate = jnp.split(z @ w_glu.astype(f32), 2, axis=-1)
    return (val * jax.nn.sigmoid(gate)).astype(u.dtype), h_re, h_im


def qkv_proj(h, w_qkv, b_qkv):
    bs, length, _ = h.shape
    qkv = h @ w_qkv + b_qkv
    nq, nk = N_HEADS * HEAD_DIM, N_KV_HEADS * HEAD_DIM
    q = qkv[..., :nq].reshape(bs, length, N_HEADS, HEAD_DIM)
    k = qkv[..., nq:nq + nk].reshape(bs, length, N_KV_HEADS, HEAD_DIM)
    v = qkv[..., nq + nk:].reshape(bs, length, N_KV_HEADS, HEAD_DIM)
    return q, k, v


def sink_softmax(s, sinks):
    sink = jnp.broadcast_to(sinks.astype(jnp.float32).reshape(N_KV_HEADS, GQA_GROUP, 1, 1),
                            s.shape[:-1] + (1,))
    p = jax.nn.softmax(jnp.concatenate([s, sink], axis=-1), axis=-1)
    return p[..., :-1]


def swa_prompt(h, w_qkv, b_qkv, sinks, w_o):
    f32 = jnp.float32
    bs, length, _ = h.shape
    n_c = length // CHUNK
    q, k, v = qkv_proj(h, w_qkv, b_qkv)
    pos = jnp.arange(length)
    q, k = rope(q, pos), rope(k, pos)
    q5 = q.reshape(bs, n_c, CHUNK, N_KV_HEADS, GQA_GROUP, HEAD_DIM).astype(f32)

    def band(t):
        tp = jnp.pad(t.astype(f32), ((0, 0), (WINDOW, 0), (0, 0), (0, 0)))
        tp = tp.reshape(bs, n_c + WIN_CHUNKS, CHUNK, N_KV_HEADS, HEAD_DIM)
        return jnp.concatenate([tp[:, j:j + n_c] for j in range(WIN_CHUNKS + 1)], axis=2)

    kb, vb = band(k), band(v)
    key_pos = ((jnp.arange(n_c)[:, None] - WIN_CHUNKS) * CHUNK
               + jnp.arange((WIN_CHUNKS + 1) * CHUNK)[None, :])
    s = jnp.einsum('bcqkgd,bcskd->bckgqs', q5, kb) * (HEAD_DIM ** -0.5)
    s = jnp.where((key_pos >= 0)[None, :, None, None, None, :], s, NEG_INF)
    p = sink_softmax(s, sinks)
    o = jnp.einsum('bckgqs,bcskd->bcqkgd', p, vb).reshape(bs, length, N_HEADS * HEAD_DIM)
    return o.astype(h.dtype) @ w_o, k[:, length - WINDOW:], v[:, length - WINDOW:]


def swa_sample(h, cache_k, cache_v, w_qkv, b_qkv, sinks, w_o):
    f32 = jnp.float32
    bs, length, _ = h.shape
    q, k, v = qkv_proj(h, w_qkv, b_qkv)
    pos = PAST_LEN + jnp.arange(length)
    q, k = rope(q, pos), rope(k, pos)
    q5 = q.reshape(bs, length, N_KV_HEADS, GQA_GROUP, HEAD_DIM).astype(f32)
    k_all = jnp.concatenate([cache_k.astype(f32), k.astype(f32)], axis=1)
    v_all = jnp.concatenate([cache_v.astype(f32), v.astype(f32)], axis=1)
    s = jnp.einsum('bqkgd,bskd->bkgqs', q5, k_all) * (HEAD_DIM ** -0.5)
    p = sink_softmax(s, sinks)
    o = jnp.einsum('bkgqs,bskd->bqkgd', p, v_all).reshape(bs, length, N_HEADS * HEAD_DIM)
    return o.astype(h.dtype) @ w_o, k, v


def route(t, w_router, router_bias):
    f32 = jnp.float32
    n = t.shape[0]
    s = jax.nn.sigmoid(t.astype(f32) @ w_router.astype(f32))
    sel = s + router_bias.astype(f32)
    per_grp = N_EXPERTS // N_EXPERT_GROUPS
    grp_score = lax.top_k(sel.reshape(n, N_EXPERT_GROUPS, per_grp), 2)[0].sum(-1)
    _, grp_idx = lax.top_k(grp_score, TOPK_GROUPS)
    grp_mask = jax.nn.one_hot(grp_idx, N_EXPERT_GROUPS, dtype=f32).sum(1) > 0
    exp_mask = jnp.repeat(grp_mask, per_grp, axis=1)
    _, idx = lax.top_k(jnp.where(exp_mask, sel, -jnp.inf), TOP_K)
    w = jnp.take_along_axis(s, idx, axis=1)
    return idx, w / jnp.sum(w, axis=-1, keepdims=True) * ROUTED_SCALE


def routed_experts(t, idx, gates, w_gate, w_up, w_down):
    n = t.shape[0]
    n_slot = n * TOP_K
    n_blocks = -(-n_slot // MOE_BLOCK) + N_EXPERTS
    flat_e = idx.reshape(n_slot).astype(jnp.int32)
    order = jnp.argsort(flat_e)
    se = flat_e[order]
    counts = jnp.bincount(flat_e, length=N_EXPERTS).astype(jnp.int32)
    padded = (counts + MOE_BLOCK - 1) // MOE_BLOCK * MOE_BLOCK
    pend = jnp.cumsum(padded)
    pstart = pend - padded
    start = jnp.cumsum(counts) - counts
    dest = pstart[se] + jnp.arange(n_slot, dtype=jnp.int32) - start[se]
    slot_tok = jnp.full((n_blocks * MOE_BLOCK,), n, jnp.int32).at[dest].set(
        (order // TOP_K).astype(jnp.int32))
    slot_gate = jnp.zeros((n_blocks * MOE_BLOCK,), jnp.float32).at[dest].set(
        gates.reshape(n_slot)[order])
    block_e = jnp.minimum(jnp.searchsorted(pend, jnp.arange(n_blocks, dtype=jnp.int32) * MOE_BLOCK,
                                           side='right'), N_EXPERTS - 1)
    t_pad = jnp.concatenate([t, jnp.zeros((1, D_MODEL), t.dtype)], axis=0)

    def expert_block(args):
        tok, gate, e = args
        rows = t_pad[tok]
        hid = jax.nn.silu(rows @ w_gate[e]) * (rows @ w_up[e])
        return (hid @ w_down[e]) * gate[:, None]

    out = lax.map(expert_block, (slot_tok.reshape(n_blocks, MOE_BLOCK),
                                 slot_gate.reshape(n_blocks, MOE_BLOCK), block_e))
    y = jnp.zeros((n + 1, D_MODEL), jnp.float32).at[slot_tok].add(
        out.reshape(n_blocks * MOE_BLOCK, D_MODEL).astype(jnp.float32))
    return y[:n].astype(t.dtype)


def moe_ffn(h, w_router, router_bias, w_gate, w_up, w_down, sw_gate, sw_up, sw_down):
    bs, length, _ = h.shape
    t = h.reshape(bs * length, D_MODEL)
    idx, gates = route(t, w_router, router_bias)
    shared = (jax.nn.silu(t @ sw_gate) * (t @ sw_up)) @ sw_down
    y = shared + routed_experts(t, idx, gates, w_gate, w_up, w_down)
    return y.reshape(bs, length, D_MODEL)


def setup_inputs(seed: int = 0) -> dict:
    key = jax.random.key(seed)
    k = jax.random.split(key, 35)
    f32 = jnp.float32

    def nrm(kk, shape, std):
        return std * jax.random.normal(kk, shape, f32)

    D = D_MODEL
    nA, nB = N_SSM_LAYERS, N_ATTN_LAYERS
    G, P, CG = SSM_GROUPS, SSM_STATE, SSM_GROUP
    E, F, FS = N_EXPERTS, EXPERT_FF, SHARED_FF
    HQ = N_HEADS * HEAD_DIM
    n_idx = jnp.arange(P, dtype=f32)
    return {
        'x_prompt': nrm(k[0], (BATCH, SEQ, D), 1.0),
        'x_sample': nrm(k[1], (DEC_BATCH, DEC_SEQ, D), 1.0),
        'c_prompt': nrm(k[2], (BATCH, D), 1.0),
        'c_sample': nrm(k[3], (DEC_BATCH, D), 1.0),
        'state_ssm_re': nrm(k[4], (nA, DEC_BATCH, G, P), 0.3),
        'state_ssm_im': nrm(k[5], (nA, DEC_BATCH, G, P), 0.3),
        'cache_swa_k': nrm(k[6], (nB, DEC_BATCH, WINDOW, N_KV_HEADS, HEAD_DIM), 1.0),
        'cache_swa_v': nrm(k[7], (nB, DEC_BATCH, WINDOW, N_KV_HEADS, HEAD_DIM), 1.0),
        'w_mod': nrm(k[8], (DEPTH, D, 6 * D), 0.3 * D ** -0.5),
        'b_mod': nrm(k[9], (DEPTH, 6 * D), 0.02),
        'g_pre_mix': 1.0 + nrm(k[10], (DEPTH, D), 0.05),
        'g_post_mix': 1.0 + nrm(k[11], (DEPTH, D), 0.05),
        'g_pre_ffn': 1.0 + nrm(k[12], (DEPTH, D), 0.05),
        'g_post_ffn': 1.0 + nrm(k[13], (DEPTH, D), 0.05),
        'ssm_a_re': -0.5 + nrm(k[14], (nA, G, P), 0.01),
        'ssm_a_im': math.pi * n_idx + nrm(k[15], (nA, G, P), 0.01),
        'ssm_log_dt': jax.random.uniform(k[16], (nA, G), f32, math.log(DT_MIN), math.log(DT_MAX)),
        'ssm_b_re': nrm(k[17], (nA, G, P, CG), (2 * CG) ** -0.5),
        'ssm_b_im': nrm(k[18], (nA, G, P, CG), (2 * CG) ** -0.5),
        'ssm_c_re': nrm(k[19], (nA, G, CG, P), 2 ** -0.5),
        'ssm_c_im': nrm(k[20], (nA, G, CG, P), 2 ** -0.5),
        'ssm_d': nrm(k[21], (nA, D), 0.5),
        'ssm_w_glu': nrm(k[22], (nA, D, 2 * D), D ** -0.5),
        'attn_w_qkv': nrm(k[23], (nB, D, QKV_DIM), D ** -0.5),
        'attn_b_qkv': nrm(k[24], (nB, QKV_DIM), 0.02),
        'attn_sinks': nrm(k[25], (nB, N_HEADS), 0.5),
        'attn_w_o': nrm(k[26], (nB, HQ, D), HQ ** -0.5),
        'moe_w_router': nrm(k[27], (DEPTH, D, E), D ** -0.5),
        'moe_router_bias': nrm(k[28], (DEPTH, E), 0.01),
        'moe_w_gate': nrm(k[29], (DEPTH, E, D, F), D ** -0.5),
        'moe_w_up': nrm(k[30], (DEPTH, E, D, F), D ** -0.5),
        'moe_w_down': nrm(k[31], (DEPTH, E, F, D), F ** -0.5),
        'sh_w_gate': nrm(k[32], (DEPTH, D, FS), D ** -0.5),
        'sh_w_up': nrm(k[33], (DEPTH, D, FS), D ** -0.5),
        'sh_w_down': nrm(k[34], (DEPTH, FS, D), FS ** -0.5),
    }


def reference(x_prompt, x_sample, c_prompt, c_sample, state_ssm_re, state_ssm_im,
              cache_swa_k, cache_swa_v, w_mod, b_mod, g_pre_mix, g_post_mix, g_pre_ffn,
              g_post_ffn, ssm_a_re, ssm_a_im, ssm_log_dt, ssm_b_re, ssm_b_im, ssm_c_re,
              ssm_c_im, ssm_d, ssm_w_glu, attn_w_qkv, attn_b_qkv, attn_sinks, attn_w_o,
              moe_w_router, moe_router_bias, moe_w_gate, moe_w_up, moe_w_down,
              sh_w_gate, sh_w_up, sh_w_down):
    xp, xs = x_prompt, x_sample
    ssm_re_p, ssm_im_p, ssm_re_s, ssm_im_s = [], [], [], []
    k_p, v_p, k_s, v_s = [], [], [], []
    for i in range(DEPTH):
        mp = modulation(c_prompt, w_mod[i], b_mod[i])
        ms = modulation(c_sample, w_mod[i], b_mod[i])
        hp = rmsnorm(xp, g_pre_mix[i]) * (1.0 + mp[1]) + mp[0]
        hs = rmsnorm(xs, g_pre_mix[i]) * (1.0 + ms[1]) + ms[0]
        j = i // N_MIXERS
        if i % N_MIXERS == 0:
            ssm_w = (ssm_a_re[j], ssm_a_im[j], ssm_log_dt[j], ssm_b_re[j], ssm_b_im[j],
                     ssm_c_re[j], ssm_c_im[j], ssm_d[j], ssm_w_glu[j])
            zero = jnp.zeros((xp.shape[0], SSM_GROUPS, SSM_STATE), jnp.float32)
            yp, hr_p, hi_p = s5_mixer(hp, zero, zero, *ssm_w)
            ys, hr_s, hi_s = s5_mixer(hs, state_ssm_re[j], state_ssm_im[j], *ssm_w)
            ssm_re_p.append(hr_p)
            ssm_im_p.append(hi_p)
            ssm_re_s.append(hr_s)
            ssm_im_s.append(hi_s)
        else:
            att_w = (attn_w_qkv[j], attn_b_qkv[j], attn_sinks[j], attn_w_o[j])
            yp, kp_new, vp_new = swa_prompt(hp, *att_w)
            ys, ks_new, vs_new = swa_sample(hs, cache_swa_k[j], cache_swa_v[j], *att_w)
            k_p.append(kp_new)
            v_p.append(vp_new)
            k_s.append(ks_new)
            v_s.append(vs_new)
        xp = xp + mp[2] * rmsnorm(yp, g_post_mix[i])
        xs = xs + ms[2] * rmsnorm(ys, g_post_mix[i])
        ffn_w = (moe_w_router[i], moe_router_bias[i], moe_w_gate[i], moe_w_up[i], moe_w_down[i],
                 sh_w_gate[i], sh_w_up[i], sh_w_down[i])
        hp = rmsnorm(xp, g_pre_ffn[i]) * (1.0 + mp[4]) + mp[3]
        hs = rmsnorm(xs, g_pre_ffn[i]) * (1.0 + ms[4]) + ms[3]
        xp = xp + mp[5] * rmsnorm(moe_ffn(hp, *ffn_w), g_post_ffn[i])
        xs = xs + ms[5] * rmsnorm(moe_ffn(hs, *ffn_w), g_post_ffn[i])
    y_prompt, y_sample = xp, xs
    new_ssm_re_prompt = jnp.stack(ssm_re_p)
    new_ssm_im_prompt = jnp.stack(ssm_im_p)
    new_ssm_re_sample = jnp.stack(ssm_re_s)
    new_ssm_im_sample = jnp.stack(ssm_im_s)
    new_k_prompt = jnp.stack(k_p)
    new_v_prompt = jnp.stack(v_p)
    new_k_sample = jnp.stack(k_s)
    new_v_sample = jnp.stack(v_s)
    return (y_prompt, y_sample, new_ssm_re_prompt, new_ssm_im_prompt, new_ssm_re_sample,
            new_ssm_im_sample, new_k_prompt, new_v_prompt, new_k_sample, new_v_sample)
```

```python
import functools
import math

import jax
import jax.numpy as jnp
from jax import lax
from jax.experimental import pallas as pl
from jax.experimental.pallas import tpu as pltpu

F32 = jnp.float32
BF16 = jnp.bfloat16
I32 = jnp.int32

D_MODEL = 1024
CHUNK = 64
SSM_GROUP = 16
SSM_GROUPS = D_MODEL // SSM_GROUP
SSM_STATE = 64
SSM_COLS = SSM_GROUPS * SSM_STATE
HEAD_DIM = 64
N_HEADS = D_MODEL // HEAD_DIM
N_KV_HEADS = 2
KV_DIM = N_KV_HEADS * HEAD_DIM
WINDOW = 128
ROPE_THETA = 10000.0
PAST_LEN = 2048
N_EXPERTS = 256
TOP_K = 8
N_EXPERT_GROUPS = 8
GROUP_SIZE = N_EXPERTS // N_EXPERT_GROUPS
TOPK_GROUPS = 4
EXPERT_FF = 256
ROUTED_SCALE = 2.5
RMS_EPS = 1e-6
NEG_INF = -1e30

SUBLANES = 8
LANES = 128

TOK_TILE = 512
S5_TILE = 256
S5_QBLK = 256
S5_NQ = D_MODEL // S5_QBLK
S5_QCOLS = S5_QBLK // SSM_GROUP * SSM_STATE
S5_CHUNK = 512
ATT_TQ = 256
MOE_BLOCK = 256
CMB_TILE = 256
VMEM_LIMIT = 56 * 1024 * 1024


def _cparams(sem):
    return pltpu.CompilerParams(dimension_semantics=sem, vmem_limit_bytes=VMEM_LIMIT)


def _rms(x, g):
    return x * lax.rsqrt(jnp.mean(x * x, axis=-1, keepdims=True) + RMS_EPS) * g


def _dot(a, b):
    return jnp.dot(a, b, preferred_element_type=F32)


def _dot_t(a, b):
    return lax.dot_general(a, b, (((1,), (1,)), ((), ())), preferred_element_type=F32)


def _mod_kernel(c_ref, w_ref, b_ref, o_ref):
    c = c_ref[...]
    a = c * jax.nn.sigmoid(c)
    o_ref[...] = jnp.dot(a, w_ref[...], preferred_element_type=F32,
                         precision=lax.Precision.HIGHEST) + b_ref[...]


def _modulation(c_all, w_mod, b_mod):
    depth = w_mod.shape[0]
    s = c_all.shape[0]
    nt = 6
    return pl.pallas_call(
        _mod_kernel,
        grid=(depth, nt),
        in_specs=[pl.BlockSpec((s, D_MODEL), lambda i, n: (0, 0)),
                  pl.BlockSpec((None, D_MODEL, D_MODEL), lambda i, n: (i, 0, n)),
                  pl.BlockSpec((None, 1, D_MODEL), lambda i, n: (i, 0, n))],
        out_specs=pl.BlockSpec((None, s, D_MODEL), lambda i, n: (i, 0, n)),
        out_shape=jax.ShapeDtypeStruct((depth, s, 6 * D_MODEL), F32),
        compiler_params=_cparams(("arbitrary", "arbitrary")),
        name="modulation",
    )(c_all, w_mod, b_mod.reshape(depth, 1, 6 * D_MODEL))


def _mod_tiles(mp_n, ms_n, tile, dec_seq):
    p = jnp.broadcast_to(mp_n[:, None, :], (mp_n.shape[0], tile, D_MODEL))
    s = jnp.repeat(ms_n, dec_seq, axis=0).reshape(-1, tile, D_MODEL)
    return jnp.concatenate([p, s], axis=0)


def _mod_tile_index(t, n_prompt_tiles, tiles_per_stream, n_streams):
    return jnp.where(t < n_prompt_tiles, t // tiles_per_stream, n_streams + t - n_prompt_tiles)


def _s5_tables(a_re, a_im, log_dt, b_re, b_im, c_re, c_im, lt):
    g, p, cg = SSM_GROUPS, SSM_STATE, SSM_GROUP
    gl = S5_QBLK // cg
    dt = jnp.exp(log_dt.astype(F32))[:, None]
    ar, ai = a_re.astype(F32), a_im.astype(F32)
    mag = jnp.exp(dt * ar)
    abar_re, abar_im = mag * jnp.cos(dt * ai), mag * jnp.sin(dt * ai)
    den = ar * ar + ai * ai
    coef_re = ((abar_re - 1.0) * ar + abar_im * ai) / den
    coef_im = (abar_im * ar - (abar_re - 1.0) * ai) / den
    br, bi = b_re.astype(F32), b_im.astype(F32)
    bbar_re = coef_re[..., None] * br - coef_im[..., None] * bi
    bbar_im = coef_re[..., None] * bi + coef_im[..., None] * br
    eye = jnp.eye(gl, dtype=F32)

    def bmat(bb):
        return jnp.einsum('qgpc,gh->qgchp', bb.reshape(S5_NQ, gl, p, cg), eye).reshape(S5_NQ, S5_QBLK, S5_QCOLS)

    def cmat(cc):
        return jnp.einsum('qgcp,gh->qgphc', cc.reshape(S5_NQ, gl, cg, p), eye).reshape(S5_NQ, S5_QCOLS, S5_QBLK)

    b_q = jnp.concatenate([bmat(bbar_re), bmat(bbar_im)], axis=-1).astype(BF16)
    c_q = jnp.concatenate([cmat(c_re.astype(F32)), -cmat(c_im.astype(F32))], axis=1).astype(BF16)

    def lay(v_re, v_im):
        lead = v_re.shape[:-2]
        r = v_re.reshape(lead + (S5_NQ, S5_QCOLS))
        i = v_im.reshape(lead + (S5_NQ, S5_QCOLS))
        return jnp.concatenate([r, i], axis=-1).reshape(lead + (2 * SSM_COLS,))

    k = jnp.arange(1, lt + 1, dtype=F32)[:, None, None]
    pmag = jnp.exp(k * (dt * ar)[None])
    pang = k * (dt * ai)[None]
    apow = lay(pmag * jnp.cos(pang), pmag * jnp.sin(pang))
    a1 = lay(abar_re, abar_im)[None]
    return b_q, c_q, a1, apow


def _s5_perm(n_grp, lt):
    r = jnp.arange(n_grp * SUBLANES * lt)
    grp, rem = r // (SUBLANES * lt), r % (SUBLANES * lt)
    i, s = rem // SUBLANES, rem % SUBLANES
    tok = grp * SUBLANES * lt + s * lt + i
    pm = (tok[:, None] == r[None, :]).astype(BF16)
    return pm, pm.T


def _s5_kernel(x_ref, shift_ref, scale_ref, gate_ref, gpre_ref, gpost_ref, pm_ref, pmt_ref,
               bq_ref, cq_ref, d_ref, wglu_ref, a1_ref, apow_ref, h0re_ref, h0im_ref,
               xo_ref, sre_ref, sim_ref,
               s_scr, y_scr, carry_scr, cinit_scr, *, lt, n_grp, chain):
    t = pl.program_id(1)
    x = x_ref[...]
    u = _rms(x, gpre_ref[...]) * (1.0 + scale_ref[...]) + shift_ref[...]
    up = _dot(pm_ref[...], u.astype(BF16))
    upb = up.astype(BF16)

    if chain:
        @pl.when(t == 0)
        def _():
            carry_scr[...] = jnp.zeros_like(carry_scr)

    half = S5_QCOLS
    for q in range(S5_NQ):
        s_scr[...] = _dot(upb[:, q * S5_QBLK:(q + 1) * S5_QBLK], bq_ref[q])
        for grp in range(n_grp):
            for cb in range(S5_QCOLS // S5_CHUNK):
                re_c = pl.ds(cb * S5_CHUNK, S5_CHUNK)
                im_c = pl.ds(half + cb * S5_CHUNK, S5_CHUNK)
                g_re = pl.ds(q * 2 * half + cb * S5_CHUNK, S5_CHUNK)
                g_im = pl.ds(q * 2 * half + half + cb * S5_CHUNK, S5_CHUNK)
                o_c = pl.ds(q * half + cb * S5_CHUNK, S5_CHUNK)
                a_re = jnp.broadcast_to(a1_ref[:, g_re], (SUBLANES, S5_CHUNK))
                a_im = jnp.broadcast_to(a1_ref[:, g_im], (SUBLANES, S5_CHUNK))
                base = grp * SUBLANES * lt

                def step(i, carry, re_c=re_c, im_c=im_c, a_re=a_re, a_im=a_im, base=base):
                    x_re, x_im = carry
                    rows = pl.ds(pl.multiple_of(base + i * SUBLANES, SUBLANES), SUBLANES)
                    n_re = a_re * x_re - a_im * x_im + s_scr[rows, re_c]
                    n_im = a_re * x_im + a_im * x_re + s_scr[rows, im_c]
                    s_scr[rows, re_c] = n_re
                    s_scr[rows, im_c] = n_im
                    return n_re, n_im

                if chain:
                    init = (jnp.zeros((SUBLANES, S5_CHUNK), F32), jnp.zeros((SUBLANES, S5_CHUNK), F32))
                else:
                    srows = pl.ds(grp * SUBLANES, SUBLANES)
                    init = (h0re_ref[srows, o_c], h0im_ref[srows, o_c])
                e_re, e_im = lax.fori_loop(0, lt, step, init, unroll=8)

                if chain:
                    al_re = apow_ref[lt - 1:lt, g_re]
                    al_im = apow_ref[lt - 1:lt, g_im]
                    c_re = carry_scr[:, g_re]
                    c_im = carry_scr[:, g_im]
                    for s in range(SUBLANES):
                        cinit_scr[s:s + 1, 0:S5_CHUNK] = c_re
                        cinit_scr[s:s + 1, S5_CHUNK:2 * S5_CHUNK] = c_im
                        c_re, c_im = (al_re * c_re - al_im * c_im + e_re[s:s + 1, :],
                                      al_re * c_im + al_im * c_re + e_im[s:s + 1, :])
                    carry_scr[:, g_re] = c_re
                    carry_scr[:, g_im] = c_im
                    ci_re = cinit_scr[:, 0:S5_CHUNK]
                    ci_im = cinit_scr[:, S5_CHUNK:2 * S5_CHUNK]

                    def fix(i, _, re_c=re_c, im_c=im_c, g_re=g_re, g_im=g_im, base=base,
                            ci_re=ci_re, ci_im=ci_im):
                        rows = pl.ds(pl.multiple_of(base + i * SUBLANES, SUBLANES), SUBLANES)
                        p_re = apow_ref[pl.ds(i, 1), g_re]
                        p_im = apow_ref[pl.ds(i, 1), g_im]
                        s_scr[rows, re_c] = s_scr[rows, re_c] + (p_re * ci_re - p_im * ci_im)
                        s_scr[rows, im_c] = s_scr[rows, im_c] + (p_re * ci_im + p_im * ci_re)
                        return 0

                    lax.fori_loop(0, lt, fix, 0, unroll=8)
                else:
                    sre_ref[srows, o_c] = e_re
                    sim_ref[srows, o_c] = e_im
        y_scr[:, q * S5_QBLK:(q + 1) * S5_QBLK] = _dot(s_scr[...].astype(BF16), cq_ref[q])

    if chain:
        @pl.when(t == pl.num_programs(1) - 1)
        def _():
            for q in range(S5_NQ):
                sre_ref[:, q * half:(q + 1) * half] = carry_scr[:, q * 2 * half:q * 2 * half + half]
                sim_ref[:, q * half:(q + 1) * half] = carry_scr[:, q * 2 * half + half:(q + 1) * 2 * half]

    y = y_scr[...] + d_ref[...] * up
    z = jax.nn.gelu(y).astype(BF16)
    vg = _dot(z, wglu_ref[...])
    o = vg[:, :D_MODEL] * jax.nn.sigmoid(vg[:, D_MODEL:])
    o_hi = o.astype(BF16)
    o_lo = (o - o_hi.astype(F32)).astype(BF16)
    ou = _dot(pmt_ref[...], o_hi) + _dot(pmt_ref[...], o_lo)
    xo_ref[...] = x + gate_ref[...] * _rms(ou, gpost_ref[...])


def _s5_call(x, shift, scale, gate, g_pre, g_post, tables, d_skip, w_glu, h0_re, h0_im, *, chain, lt, n_grp):
    b_q, c_q, a1, apow = tables
    pm, pmt = _s5_perm(n_grp, lt)
    tile = n_grp * SUBLANES * lt
    cols2 = 2 * SSM_COLS
    if chain:
        n_seq, length, _ = x.shape
        grid = (n_seq, length // tile)
        xf = x.reshape(n_seq * length, D_MODEL)
        nt = grid[1]
        tok_map = lambda b, t: (b * nt + t, 0)
        mod_spec = pl.BlockSpec((None, 1, D_MODEL), lambda b, t: (b, 0, 0))
        mods = [m.reshape(n_seq, 1, D_MODEL) for m in (shift, scale, gate)]
        h0_spec = pl.BlockSpec((SUBLANES, SSM_COLS), lambda b, t: (0, 0))
        h0_re = jnp.zeros((SUBLANES, SSM_COLS), F32)
        h0_im = h0_re
        st_spec = pl.BlockSpec((None, 1, SSM_COLS), lambda b, t: (b, 0, 0))
        st_shape = jax.ShapeDtypeStruct((n_seq, 1, SSM_COLS), F32)
    else:
        n_seq, length, _ = x.shape
        assert length == lt
        xf = x.reshape(n_seq * length, D_MODEL)
        grid = (1, n_seq * length // tile)
        tok_map = lambda b, t: (t, 0)
        mod_spec = pl.BlockSpec((tile, D_MODEL), tok_map)
        mods = [jnp.repeat(m, length, axis=0) for m in (shift, scale, gate)]
        spt = n_grp * SUBLANES
        h0_spec = pl.BlockSpec((spt, SSM_COLS), lambda b, t: (t, 0))
        st_spec = pl.BlockSpec((spt, SSM_COLS), lambda b, t: (t, 0))
        st_shape = jax.ShapeDtypeStruct((n_seq, SSM_COLS), F32)
    const2 = lambda shp: pl.BlockSpec(shp, lambda b, t: (0, 0))
    const3 = lambda shp: pl.BlockSpec(shp, lambda b, t: (0, 0, 0))
    kern = functools.partial(_s5_kernel, lt=lt, n_grp=n_grp, chain=chain)
    xo, s_re, s_im = pl.pallas_call(
        kern,
        grid=grid,
        in_specs=[pl.BlockSpec((tile, D_MODEL), tok_map), mod_spec, mod_spec, mod_spec,
                  const2((1, D_MODEL)), const2((1, D_MODEL)),
                  const2((tile, tile)), const2((tile, tile)),
                  const3(b_q.shape), const3(c_q.shape),
                  const2((1, D_MODEL)), const2(w_glu.shape),
                  const2((1, cols2)), const2((lt, cols2)),
                  h0_spec, h0_spec],
        out_specs=[pl.BlockSpec((tile, D_MODEL), tok_map), st_spec, st_spec],
        out_shape=[jax.ShapeDtypeStruct(xf.shape, F32), st_shape, st_shape],
        scratch_shapes=[pltpu.VMEM((tile, 2 * S5_QCOLS), F32),
                        pltpu.VMEM((tile, D_MODEL), F32),
                        pltpu.VMEM((1, cols2), F32),
                        pltpu.VMEM((SUBLANES, 2 * S5_CHUNK), F32)],
        compiler_params=_cparams(("arbitrary", "arbitrary")),
        name="s5_chain" if chain else "s5_streams",
    )(xf, *mods, g_pre.reshape(1, D_MODEL), g_post.reshape(1, D_MODEL), pm, pmt, b_q, c_q,
      d_skip.reshape(1, D_MODEL), w_glu, a1, apow, h0_re, h0_im)
    return xo, s_re.reshape(n_seq, SSM_GROUPS, SSM_STATE), s_im.reshape(n_seq, SSM_GROUPS, SSM_STATE)


def _rope_partner(w):
    lead = w.shape[:-1]
    w4 = w.reshape(lead + (-1, 2, HEAD_DIM // 2))
    return jnp.concatenate([-w4[..., 1:2, :], w4[..., 0:1, :]], axis=-2).reshape(w.shape)


def _qkv_kernel(x_ref, shift_ref, scale_ref, gpre_ref, wq_ref, wqp_ref, bq_ref, bqp_ref,
                wk_ref, wkp_ref, bk_ref, bkp_ref, wv_ref, bv_ref, cos_ref, sin_ref,
                q_ref, k_ref, v_ref):
    h = (_rms(x_ref[...], gpre_ref[...]) * (1.0 + scale_ref[...]) + shift_ref[...]).astype(BF16)
    cos = cos_ref[...]
    sin = sin_ref[...]
    cos_q = jnp.concatenate([cos] * (D_MODEL // LANES), axis=1)
    sin_q = jnp.concatenate([sin] * (D_MODEL // LANES), axis=1)
    q = (_dot(h, wq_ref[...]) + bq_ref[...]) * cos_q + (_dot(h, wqp_ref[...]) + bqp_ref[...]) * sin_q
    q_ref[...] = (q * (HEAD_DIM ** -0.5)).astype(BF16)
    k_ref[...] = (_dot(h, wk_ref[...]) + bk_ref[...]) * cos + (_dot(h, wkp_ref[...]) + bkp_ref[...]) * sin
    v_ref[...] = _dot(h, wv_ref[...]) + bv_ref[...]


def _qkv_call(x_all, shift_t, scale_t, g_pre, w_qkv, b_qkv, cos_all, sin_all, mod_index):
    n = x_all.shape[0]
    nq = N_HEADS * HEAD_DIM
    wq, wk, wv = w_qkv[:, :nq], w_qkv[:, nq:nq + KV_DIM], w_qkv[:, nq + KV_DIM:]
    bq, bk, bv = b_qkv[:nq], b_qkv[nq:nq + KV_DIM], b_qkv[nq + KV_DIM:]
    tok = lambda t: (t, 0)
    modm = lambda t: (mod_index(t), 0, 0)
    c2 = lambda shp: pl.BlockSpec(shp, lambda t: (0, 0))
    return pl.pallas_call(
        _qkv_kernel,
        grid=(n // TOK_TILE,),
        in_specs=[pl.BlockSpec((TOK_TILE, D_MODEL), tok),
                  pl.BlockSpec((None, TOK_TILE, D_MODEL), modm),
                  pl.BlockSpec((None, TOK_TILE, D_MODEL), modm),
                  c2((1, D_MODEL)),
                  c2((D_MODEL, nq)), c2((D_MODEL, nq)), c2((1, nq)), c2((1, nq)),
                  c2((D_MODEL, KV_DIM)), c2((D_MODEL, KV_DIM)), c2((1, KV_DIM)), c2((1, KV_DIM)),
                  c2((D_MODEL, KV_DIM)), c2((1, KV_DIM)),
                  pl.BlockSpec((TOK_TILE, LANES), tok), pl.BlockSpec((TOK_TILE, LANES), tok)],
        out_specs=[pl.BlockSpec((TOK_TILE, nq), tok), pl.BlockSpec((TOK_TILE, KV_DIM), tok),
                   pl.BlockSpec((TOK_TILE, KV_DIM), tok)],
        out_shape=[jax.ShapeDtypeStruct((n, nq), BF16), jax.ShapeDtypeStruct((n, KV_DIM), F32),
                   jax.ShapeDtypeStruct((n, KV_DIM), F32)],
        compiler_params=_cparams(("arbitrary",)),
        name="qkv_rope",
    )(x_all, shift_t, scale_t, g_pre.reshape(1, D_MODEL),
      wq.astype(BF16), _rope_partner(wq).astype(BF16), bq.reshape(1, nq), _rope_partner(bq).reshape(1, nq),
      wk.astype(BF16), _rope_partner(wk).astype(BF16), bk.reshape(1, KV_DIM), _rope_partner(bk).reshape(1, KV_DIM),
      wv.astype(BF16), bv.reshape(1, KV_DIM), cos_all, sin_all)


def _attend(q, k, v, sink_ref, valid, o_ref):
    kb = k.astype(BF16)
    vb = v.astype(BF16)
    ks = pltpu.roll(k, HEAD_DIM, axis=1).astype(BF16)
    vs = pltpu.roll(v, HEAD_DIM, axis=1).astype(BF16)
    lo = lax.broadcasted_iota(I32, kb.shape, 1) < HEAD_DIM
    zero = jnp.zeros_like(kb)
    k_sel = ((jnp.where(lo, kb, zero), jnp.where(lo, zero, ks)),
             (jnp.where(lo, ks, zero), jnp.where(lo, zero, kb)))
    v_sel = ((jnp.where(lo, vb, zero), jnp.where(lo, zero, vs)),
             (jnp.where(lo, vs, zero), jnp.where(lo, zero, vb)))
    group = N_HEADS // N_KV_HEADS
    for pair in range(N_HEADS // 2):
        qp = q[:, pair * LANES:(pair + 1) * LANES]
        kvh = (2 * pair) // group
        acc = None
        for half in range(2):
            sink = sink_ref[2 * pair + half]
            s = _dot_t(qp, k_sel[kvh][half])
            if valid is not None:
                s = jnp.where(valid, s, NEG_INF)
            m = jnp.maximum(jnp.max(s, axis=1, keepdims=True), sink)
            e = jnp.exp(s - m)
            den = jnp.sum(e, axis=1, keepdims=True) + jnp.exp(sink - m)
            p = (e / den).astype(BF16)
            pv = _dot(p, v_sel[kvh][half])
            acc = pv if acc is None else acc + pv
        o_ref[:, pair * LANES:(pair + 1) * LANES] = acc.astype(o_ref.dtype)


def _attn_prompt_kernel(sink_ref, q_ref, kp_ref, kc_ref, vp_ref, vc_ref, o_ref):
    t = pl.program_id(1)
    k = jnp.concatenate([kp_ref[...], kc_ref[...]], axis=0)
    v = jnp.concatenate([vp_ref[...], vc_ref[...]], axis=0)
    n_key = WINDOW + ATT_TQ
    qc = lax.broadcasted_iota(I32, (ATT_TQ, n_key), 0) // CHUNK
    kc = lax.broadcasted_iota(I32, (ATT_TQ, n_key), 1) // CHUNK
    first = t * (ATT_TQ // CHUNK) - WINDOW // CHUNK
    valid = (kc >= qc) & (kc <= qc + WINDOW // CHUNK) & (kc + first >= 0)
    _attend(q_ref[...], k, v, sink_ref, valid, o_ref)


def _attn_prompt_call(q, k, v, sinks, n_seq, length):
    nt = length // ATT_TQ
    wpt = ATT_TQ // WINDOW
    qmap = lambda b, t, s: (b * nt + t, 0)
    pmap = lambda b, t, s: (b * nt * wpt + jnp.maximum(t * wpt - 1, 0), 0)
    return pl.pallas_call(
        _attn_prompt_kernel,
        grid_spec=pltpu.PrefetchScalarGridSpec(
            num_scalar_prefetch=1,
            grid=(n_seq, nt),
            in_specs=[pl.BlockSpec((ATT_TQ, D_MODEL), qmap),
                      pl.BlockSpec((WINDOW, KV_DIM), pmap), pl.BlockSpec((ATT_TQ, KV_DIM), qmap),
                      pl.BlockSpec((WINDOW, KV_DIM), pmap), pl.BlockSpec((ATT_TQ, KV_DIM), qmap)],
            out_specs=pl.BlockSpec((ATT_TQ, D_MODEL), qmap)),
        out_shape=jax.ShapeDtypeStruct(q.shape, BF16),
        compiler_params=_cparams(("arbitrary", "arbitrary")),
        name="attn_prompt",
    )(sinks, q, k, k, v, v)


def _attn_sample_kernel(sink_ref, q_ref, ck_ref, kn_ref, cv_ref, vn_ref, o_ref):
    length = kn_ref.shape[0]
    fill = jnp.zeros((WINDOW - length, KV_DIM), F32)
    k = jnp.concatenate([ck_ref[...], kn_ref[...], fill], axis=0)
    v = jnp.concatenate([cv_ref[...], vn_ref[...], fill], axis=0)
    valid = lax.broadcasted_iota(I32, (length, 2 * WINDOW), 1) < WINDOW + length
    _attend(q_ref[...], k, v, sink_ref, valid, o_ref)


def _attn_sample_call(q, k, v, cache_k, cache_v, sinks, n_seq, length):
    tok = lambda b, s: (b, 0)
    return pl.pallas_call(
        _attn_sample_kernel,
        grid_spec=pltpu.PrefetchScalarGridSpec(
            num_scalar_prefetch=1,
            grid=(n_seq,),
            in_specs=[pl.BlockSpec((length, D_MODEL), tok),
                      pl.BlockSpec((WINDOW, KV_DIM), tok), pl.BlockSpec((length, KV_DIM), tok),
                      pl.BlockSpec((WINDOW, KV_DIM), tok), pl.BlockSpec((length, KV_DIM), tok)],
            out_specs=pl.BlockSpec((length, D_MODEL), tok)),
        out_shape=jax.ShapeDtypeStruct(q.shape, BF16),
        compiler_params=_cparams(("arbitrary",)),
        name="attn_sample",
    )(sinks, q, cache_k, k, cache_v, v)


def _proj_kernel(x_ref, o_ref, gate_ref, gpost_ref, wo_ref, xo_ref):
    y = _dot(o_ref[...], wo_ref[...])
    xo_ref[...] = x_ref[...] + gate_ref[...] * _rms(y, gpost_ref[...])


def _proj_call(x_all, o_all, gate_t, g_post, w_o, mod_index):
    n = x_all.shape[0]
    tok = lambda t: (t, 0)
    return pl.pallas_call(
        _proj_kernel,
        grid=(n // TOK_TILE,),
        in_specs=[pl.BlockSpec((TOK_TILE, D_MODEL), tok), pl.BlockSpec((TOK_TILE, D_MODEL), tok),
                  pl.BlockSpec((None, TOK_TILE, D_MODEL), lambda t: (mod_index(t), 0, 0)),
                  pl.BlockSpec((1, D_MODEL), lambda t: (0, 0)),
                  pl.BlockSpec((D_MODEL, D_MODEL), lambda t: (0, 0))],
        out_specs=pl.BlockSpec((TOK_TILE, D_MODEL), tok),
        out_shape=jax.ShapeDtypeStruct(x_all.shape, F32),
        compiler_params=_cparams(("arbitrary",)),
        name="attn_proj",
    )(x_all, o_all, gate_t, g_post.reshape(1, D_MODEL), w_o.astype(BF16))


def _router_kernel(x_ref, shift_ref, scale_ref, gpre_ref, wr_hi_ref, wr_lo_ref, rb_ref,
                   swg_ref, swu_ref, swd_ref, h_ref, sh_ref, idx_ref, gate_ref):
    h = _rms(x_ref[...], gpre_ref[...]) * (1.0 + scale_ref[...]) + shift_ref[...]
    h_ref[...] = h
    hb = h.astype(BF16)
    hl = (h - hb.astype(F32)).astype(BF16)
    logits = _dot_t(wr_hi_ref[...], hb) + _dot_t(wr_hi_ref[...], hl) + _dot_t(wr_lo_ref[...], hb)
    s = jax.nn.sigmoid(logits)
    sel = s + rb_ref[...]
    nt = sel.shape[1]
    ninf = jnp.float32(-jnp.inf)

    gi = lax.broadcasted_iota(I32, (GROUP_SIZE, nt), 0)
    grp_sel, gs = [], []
    for g in range(N_EXPERT_GROUPS):
        sg = sel[g * GROUP_SIZE:(g + 1) * GROUP_SIZE, :]
        m1 = jnp.max(sg, axis=0, keepdims=True)
        first = jnp.min(jnp.where(sg == m1, gi, GROUP_SIZE), axis=0, keepdims=True)
        m2 = jnp.max(jnp.where(gi == first, ninf, sg), axis=0, keepdims=True)
        grp_sel.append(sg)
        gs.append(m1 + m2)
    parts = []
    for g in range(N_EXPERT_GROUPS):
        rank = jnp.zeros((1, nt), I32)
        for j in range(N_EXPERT_GROUPS):
            if j != g:
                beats = (gs[j] >= gs[g]) if j < g else (gs[j] > gs[g])
                rank = rank + beats.astype(I32)
        parts.append(jnp.where(rank < TOPK_GROUPS, grp_sel[g], ninf))
    masked = jnp.concatenate(parts, axis=0)

    ei = lax.broadcasted_iota(I32, masked.shape, 0)
    ws = []
    for k in range(TOP_K):
        m = jnp.max(masked, axis=0, keepdims=True)
        ik = jnp.min(jnp.where(masked == m, ei, N_EXPERTS), axis=0, keepdims=True)
        hit = ei == ik
        ws.append(jnp.sum(jnp.where(hit, s, 0.0), axis=0, keepdims=True))
        masked = jnp.where(hit, ninf, masked)
        idx_ref[k:k + 1, :] = ik
    tot = ws[0]
    for k in range(1, TOP_K):
        tot = tot + ws[k]
    for k in range(TOP_K):
        gate_ref[k:k + 1, :] = ws[k] / tot * ROUTED_SCALE

    a = _dot(hb, swg_ref[...])
    b = _dot(hb, swu_ref[...])
    hid = (a * jax.nn.sigmoid(a) * b).astype(BF16)
    sh_ref[...] = _dot(hid, swd_ref[...])


def _router_call(x_all, shift_t, scale_t, g_pre, w_router, router_bias, sw_gate, sw_up, sw_down, mod_index):
    n = x_all.shape[0]
    tok = lambda t: (t, 0)
    modm = lambda t: (mod_index(t), 0, 0)
    c2 = lambda shp: pl.BlockSpec(shp, lambda t: (0, 0))
    wr_t = w_router.astype(F32).T
    wr_hi = wr_t.astype(BF16)
    wr_lo = (wr_t - wr_hi.astype(F32)).astype(BF16)
    ff = sw_gate.shape[1]
    return pl.pallas_call(
        _router_kernel,
        grid=(n // TOK_TILE,),
        in_specs=[pl.BlockSpec((TOK_TILE, D_MODEL), tok),
                  pl.BlockSpec((None, TOK_TILE, D_MODEL), modm),
                  pl.BlockSpec((None, TOK_TILE, D_MODEL), modm),
                  c2((1, D_MODEL)), c2((N_EXPERTS, D_MODEL)), c2((N_EXPERTS, D_MODEL)), c2((N_EXPERTS, 1)),
                  c2((D_MODEL, ff)), c2((D_MODEL, ff)), c2((ff, D_MODEL))],
        out_specs=[pl.BlockSpec((TOK_TILE, D_MODEL), tok), pl.BlockSpec((TOK_TILE, D_MODEL), tok),
                   pl.BlockSpec((TOP_K, TOK_TILE), lambda t: (0, t)),
                   pl.BlockSpec((TOP_K, TOK_TILE), lambda t: (0, t))],
        out_shape=[jax.ShapeDtypeStruct((n, D_MODEL), F32), jax.ShapeDtypeStruct((n, D_MODEL), F32),
                   jax.ShapeDtypeStruct((TOP_K, n), I32), jax.ShapeDtypeStruct((TOP_K, n), F32)],
        compiler_params=_cparams(("arbitrary",)),
        name="moe_router",
    )(x_all, shift_t, scale_t, g_pre.reshape(1, D_MODEL), wr_hi, wr_lo,
      router_bias.astype(F32).reshape(N_EXPERTS, 1),
      sw_gate.astype(BF16), sw_up.astype(BF16), sw_down.astype(BF16))


def _rank_kernel(idx_ref, dest_ref, cnt_ref, cnt_col, cnt_row, base_col):
    p = pl.program_id(0)
    t = pl.program_id(1)
    nt = idx_ref.shape[1]
    ei = lax.broadcasted_iota(I32, (N_EXPERTS, nt), 0)
    idx = idx_ref[...]
    hits = [ei == idx[k:k + 1, :] for k in range(TOP_K)]
    onehot = hits[0].astype(F32)
    for k in range(1, TOP_K):
        onehot = onehot + hits[k].astype(F32)
    tile_cnt = jnp.sum(onehot, axis=1, keepdims=True)

    @pl.when((p == 0) & (t == 0))
    def _():
        cnt_col[...] = jnp.zeros_like(cnt_col)
        cnt_row[...] = jnp.zeros_like(cnt_row)

    @pl.when(p == 0)
    def _():
        cnt_col[...] += tile_cnt
        cnt_row[...] += _dot_t(jnp.ones((SUBLANES, nt), BF16), onehot.astype(BF16))
        dest_ref[...] = jnp.zeros_like(dest_ref)

    @pl.when((p == 1) & (t == 0))
    def _():
        cnt = cnt_row[0:1, :]
        padded = jnp.floor((cnt + (MOE_BLOCK - 1)) / MOE_BLOCK) * MOE_BLOCK
        li = lax.broadcasted_iota(I32, (N_EXPERTS, N_EXPERTS), 1)
        si = lax.broadcasted_iota(I32, (N_EXPERTS, N_EXPERTS), 0)
        base_col[...] = jnp.sum(jnp.where(li < si, padded, 0.0), axis=1, keepdims=True)
        cnt_ref[...] = cnt_row[...]

    @pl.when(p == 1)
    def _():
        r = lax.broadcasted_iota(I32, (nt, nt), 0)
        c = lax.broadcasted_iota(I32, (nt, nt), 1)
        before = (r < c).astype(BF16)
        pos = base_col[...] + _dot(onehot.astype(BF16), before)
        for k in range(TOP_K):
            dest_ref[k:k + 1, :] = jnp.sum(jnp.where(hits[k], pos, 0.0), axis=0, keepdims=True).astype(I32)
        base_col[...] += tile_cnt


def _rank_call(idx):
    n = idx.shape[1]
    return pl.pallas_call(
        _rank_kernel,
        grid=(2, n // TOK_TILE),
        in_specs=[pl.BlockSpec((TOP_K, TOK_TILE), lambda p, t: (0, t))],
        out_specs=[pl.BlockSpec((TOP_K, TOK_TILE), lambda p, t: (0, t * p)),
                   pl.BlockSpec((SUBLANES, N_EXPERTS), lambda p, t: (0, 0))],
        out_shape=[jax.ShapeDtypeStruct((TOP_K, n), I32), jax.ShapeDtypeStruct((SUBLANES, N_EXPERTS), F32)],
        scratch_shapes=[pltpu.VMEM((N_EXPERTS, 1), F32), pltpu.VMEM((SUBLANES, N_EXPERTS), F32),
                        pltpu.VMEM((N_EXPERTS, 1), F32)],
        compiler_params=_cparams(("arbitrary", "arbitrary")),
        name="moe_rank",
    )(idx)


def _dispatch_kernel(dest_ref, h_ref, xs_ref, sem):
    nt = h_ref.shape[0]

    def issue(j, _):
        for k in range(TOP_K):
            pltpu.make_async_copy(h_ref.at[pl.ds(j, 1), :],
                                  xs_ref.at[pl.ds(dest_ref[k, j], 1), :], sem).start()
        return 0

    lax.fori_loop(0, nt, issue, 0, unroll=4)
    for k in range(TOP_K):
        pltpu.make_async_copy(h_ref, xs_ref.at[pl.ds(0, nt), :], sem).wait()


def _dispatch_call(h, dest, n_rows):
    n = h.shape[0]
    return pl.pallas_call(
        _dispatch_kernel,
        grid=(n // TOK_TILE,),
        in_specs=[pl.BlockSpec((TOP_K, TOK_TILE), lambda t: (0, t), memory_space=pltpu.SMEM),
                  pl.BlockSpec((TOK_TILE, D_MODEL), lambda t: (t, 0))],
        out_specs=pl.BlockSpec(memory_space=pl.ANY),
        out_shape=jax.ShapeDtypeStruct((n_rows, D_MODEL), F32),
        scratch_shapes=[pltpu.SemaphoreType.DMA(())],
        compiler_params=_cparams(("arbitrary",)),
        name="moe_dispatch",
    )(dest, h)


def _expert_kernel(be_ref, nv_ref, nu_ref, xs_ref, wg_ref, wu_ref, wd_ref, ys_ref, wg_b, wu_b, wd_b):
    b = pl.program_id(0)

    @pl.when(b < nu_ref[0])
    def _():
        prev = be_ref[jnp.maximum(b - 1, 0)]

        @pl.when((b == 0) | (be_ref[b] != prev))
        def _():
            wg_b[...] = wg_ref[...].astype(BF16)
            wu_b[...] = wu_ref[...].astype(BF16)
            wd_b[...] = wd_ref[...].astype(BF16)

        rid = lax.broadcasted_iota(I32, (MOE_BLOCK, 1), 0)
        rows = jnp.where(rid < nv_ref[b], xs_ref[...], 0.0).astype(BF16)
        a = _dot(rows, wg_b[...])
        u = _dot(rows, wu_b[...])
        hid = (a * jax.nn.sigmoid(a) * u).astype(BF16)
        ys_ref[...] = _dot(hid, wd_b[...])


def _expert_call(xs, block_e, n_valid, n_used, w_gate, w_up, w_down):
    n_rows = xs.shape[0]
    n_blocks = n_rows // MOE_BLOCK
    ff = w_gate.shape[2]
    row = lambda b, be, nv, nu: (jnp.minimum(b, nu[0] - 1), 0)
    wmap = lambda b, be, nv, nu: (be[b], 0, 0)
    return pl.pallas_call(
        _expert_kernel,
        grid_spec=pltpu.PrefetchScalarGridSpec(
            num_scalar_prefetch=3,
            grid=(n_blocks,),
            in_specs=[pl.BlockSpec((MOE_BLOCK, D_MODEL), row),
                      pl.BlockSpec((None, D_MODEL, ff), wmap), pl.BlockSpec((None, D_MODEL, ff), wmap),
                      pl.BlockSpec((None, ff, D_MODEL), wmap)],
            out_specs=pl.BlockSpec((MOE_BLOCK, D_MODEL), row),
            scratch_shapes=[pltpu.VMEM((D_MODEL, ff), BF16), pltpu.VMEM((D_MODEL, ff), BF16),
                            pltpu.VMEM((ff, D_MODEL), BF16)]),
        out_shape=jax.ShapeDtypeStruct((n_rows, D_MODEL), F32),
        compiler_params=_cparams(("arbitrary",)),
        name="moe_experts",
    )(block_e, n_valid, n_used, xs, w_gate, w_up, w_down)


def _combine_kernel(dcur_ref, dnext_ref, x_ref, sh_ref, g_ref, gate_ref, gpost_ref, ys_ref, xo_ref, ybuf, sem):
    t = pl.program_id(0)
    n_t = pl.num_programs(0)
    nt = x_ref.shape[0]
    slot = t % 2

    def gather(d_ref, to):
        def issue(j, _):
            for k in range(TOP_K):
                pltpu.make_async_copy(ys_ref.at[pl.ds(d_ref[k, j], 1), :],
                                      ybuf.at[to, k, pl.ds(j, 1), :], sem.at[to]).start()
            return 0
        lax.fori_loop(0, nt, issue, 0, unroll=4)

    @pl.when(t == 0)
    def _():
        gather(dcur_ref, 0)

    @pl.when(t + 1 < n_t)
    def _():
        gather(dnext_ref, 1 - slot)

    for k in range(TOP_K):
        pltpu.make_async_copy(ys_ref.at[pl.ds(0, nt), :], ybuf.at[slot, k], sem.at[slot]).wait()

    g = g_ref[...]
    acc = sh_ref[...]
    for k in range(TOP_K):
        acc = acc + g[:, k:k + 1] * ybuf[slot, k]
    xo_ref[...] = x_ref[...] + gate_ref[...] * _rms(acc, gpost_ref[...])


def _combine_call(x_all, shared, gates_t, dest, ys, gate_t, g_post, mod_index):
    n = x_all.shape[0]
    n_t = n // CMB_TILE
    tok = lambda t: (t, 0)
    return pl.pallas_call(
        _combine_kernel,
        grid=(n_t,),
        in_specs=[pl.BlockSpec((TOP_K, CMB_TILE), lambda t: (0, t), memory_space=pltpu.SMEM),
                  pl.BlockSpec((TOP_K, CMB_TILE), lambda t: (0, jnp.minimum(t + 1, n_t - 1)),
                               memory_space=pltpu.SMEM),
                  pl.BlockSpec((CMB_TILE, D_MODEL), tok), pl.BlockSpec((CMB_TILE, D_MODEL), tok),
                  pl.BlockSpec((CMB_TILE, TOP_K), tok),
                  pl.BlockSpec((None, CMB_TILE, D_MODEL), lambda t: (mod_index(t), 0, 0)),
                  pl.BlockSpec((1, D_MODEL), lambda t: (0, 0)),
                  pl.BlockSpec(memory_space=pl.ANY)],
        out_specs=pl.BlockSpec((CMB_TILE, D_MODEL), tok),
        out_shape=jax.ShapeDtypeStruct(x_all.shape, F32),
        scratch_shapes=[pltpu.VMEM((2, TOP_K, CMB_TILE, D_MODEL), F32), pltpu.SemaphoreType.DMA((2,))],
        compiler_params=_cparams(("arbitrary",)),
        name="moe_combine",
    )(dest, dest, x_all, shared, gates_t, gate_t, g_post.reshape(1, D_MODEL), ys)


def _moe_layer(x_all, mods, g_pre, g_post, w_router, router_bias, w_gate, w_up, w_down,
               sw_gate, sw_up, sw_down, idx_tok, idx_cmb):
    shift_t, scale_t, gate_t = mods
    n = x_all.shape[0]
    h, shared, idx, gates = _router_call(x_all, shift_t, scale_t, g_pre, w_router, router_bias,
                                         sw_gate, sw_up, sw_down, idx_tok)
    dest, cnt = _rank_call(idx)
    n_blocks = (n * TOP_K) // MOE_BLOCK + N_EXPERTS
    counts = cnt[0].astype(I32)
    padded = (counts + MOE_BLOCK - 1) // MOE_BLOCK * MOE_BLOCK
    pend = jnp.cumsum(padded)
    pstart = pend - padded
    blk_start = jnp.arange(n_blocks, dtype=I32) * MOE_BLOCK
    block_e = jnp.minimum(jnp.searchsorted(pend, blk_start, side='right'), N_EXPERTS - 1).astype(I32)
    n_valid = jnp.clip(counts[block_e] - (blk_start - pstart[block_e]), 0, MOE_BLOCK).astype(I32)
    n_used = (pend[-1:] // MOE_BLOCK).astype(I32)
    xs = _dispatch_call(h, dest, n_blocks * MOE_BLOCK)
    ys = _expert_call(xs, block_e, n_valid, n_used, w_gate, w_up, w_down)
    return _combine_call(x_all, shared, gates.T, dest, ys, gate_t, g_post, idx_cmb)


def kernel(x_prompt, x_sample, c_prompt, c_sample, state_ssm_re, state_ssm_im, cache_swa_k, cache_swa_v, w_mod, b_mod, g_pre_mix, g_post_mix, g_pre_ffn, g_post_ffn, ssm_a_re, ssm_a_im, ssm_log_dt, ssm_b_re, ssm_b_im, ssm_c_re, ssm_c_im, ssm_d, ssm_w_glu, attn_w_qkv, attn_b_qkv, attn_sinks, attn_w_o, moe_w_router, moe_router_bias, moe_w_gate, moe_w_up, moe_w_down, sh_w_gate, sh_w_up, sh_w_down):
    n_b, seq, _ = x_prompt.shape
    n_db, dec_seq, _ = x_sample.shape
    depth = w_mod.shape[0]
    n_p, n_s = n_b * seq, n_db * dec_seq
    n = n_p + n_s
    assert seq % TOK_TILE == 0 and n_s % TOK_TILE == 0 and dec_seq * SUBLANES <= S5_TILE

    c_all = jnp.concatenate([c_prompt, c_sample], axis=0)
    n_str = n_b + n_db
    pad = (-n_str) % SUBLANES
    mod = _modulation(jnp.pad(c_all, ((0, pad), (0, 0))), w_mod, b_mod)[:, :n_str]
    mod = mod.reshape(depth, n_str, 6, D_MODEL)

    def tile_index(tile):
        return functools.partial(_mod_tile_index, n_prompt_tiles=n_p // tile,
                                 tiles_per_stream=seq // tile, n_streams=n_b)

    idx_tok, idx_cmb = tile_index(TOK_TILE), tile_index(CMB_TILE)

    half = HEAD_DIM // 2
    inv = ROPE_THETA ** (-jnp.arange(half, dtype=F32) / half)
    pos = jnp.concatenate([jnp.tile(jnp.arange(seq), n_b), jnp.tile(PAST_LEN + jnp.arange(dec_seq), n_db)])
    ang = pos.astype(F32)[:, None] * jnp.tile(inv, LANES // half)[None, :]
    cos_all, sin_all = jnp.cos(ang), jnp.sin(ang)

    x_all = jnp.concatenate([x_prompt.reshape(n_p, D_MODEL), x_sample.reshape(n_s, D_MODEL)], axis=0)
    ssm_p, ssm_s, kv_p, kv_s = [], [], [], []
    for i in range(depth):
        mp, ms = mod[i, :n_b], mod[i, n_b:]
        j = i // 2
        if i % 2 == 0:
            lt_p = S5_TILE // SUBLANES
            w_glu = ssm_w_glu[j].astype(BF16)
            ssm_w = (ssm_a_re[j], ssm_a_im[j], ssm_log_dt[j], ssm_b_re[j], ssm_b_im[j], ssm_c_re[j], ssm_c_im[j])
            xp, re_p, im_p = _s5_call(
                x_all[:n_p].reshape(n_b, seq, D_MODEL), mp[:, 0], mp[:, 1], mp[:, 2],
                g_pre_mix[i], g_post_mix[i], _s5_tables(*ssm_w, lt_p), ssm_d[j], w_glu, None, None,
                chain=True, lt=lt_p, n_grp=1)
            xs_, re_s, im_s = _s5_call(
                x_all[n_p:].reshape(n_db, dec_seq, D_MODEL), ms[:, 0], ms[:, 1], ms[:, 2],
                g_pre_mix[i], g_post_mix[i], _s5_tables(*ssm_w, dec_seq), ssm_d[j], w_glu,
                state_ssm_re[j].reshape(n_db, SSM_COLS).astype(F32),
                state_ssm_im[j].reshape(n_db, SSM_COLS).astype(F32),
                chain=False, lt=dec_seq, n_grp=S5_TILE // (SUBLANES * dec_seq))
            x_all = jnp.concatenate([xp, xs_], axis=0)
            ssm_p.append((re_p, im_p))
            ssm_s.append((re_s, im_s))
        else:
            shift_t = _mod_tiles(mp[:, 0], ms[:, 0], TOK_TILE, dec_seq)
            scale_t = _mod_tiles(mp[:, 1], ms[:, 1], TOK_TILE, dec_seq)
            gate_t = _mod_tiles(mp[:, 2], ms[:, 2], TOK_TILE, dec_seq)
            q, k, v = _qkv_call(x_all, shift_t, scale_t, g_pre_mix[i], attn_w_qkv[j], attn_b_qkv[j],
                                cos_all, sin_all, idx_tok)
            sinks = attn_sinks[j].astype(F32)
            o_p = _attn_prompt_call(q[:n_p], k[:n_p], v[:n_p], sinks, n_b, seq)
            o_s = _attn_sample_call(q[n_p:], k[n_p:], v[n_p:],
                                    cache_swa_k[j].reshape(n_db * WINDOW, KV_DIM).astype(F32),
                                    cache_swa_v[j].reshape(n_db * WINDOW, KV_DIM).astype(F32),
                                    sinks, n_db, dec_seq)
            x_all = _proj_call(x_all, jnp.concatenate([o_p, o_s], axis=0), gate_t, g_post_mix[i],
                               attn_w_o[j], idx_tok)
            kp4 = k[:n_p].reshape(n_b, seq, N_KV_HEADS, HEAD_DIM)
            vp4 = v[:n_p].reshape(n_b, seq, N_KV_HEADS, HEAD_DIM)
            kv_p.append((kp4[:, seq - WINDOW:], vp4[:, seq - WINDOW:]))
            kv_s.append((k[n_p:].reshape(n_db, dec_seq, N_KV_HEADS, HEAD_DIM),
                         v[n_p:].reshape(n_db, dec_seq, N_KV_HEADS, HEAD_DIM)))
        mods = (_mod_tiles(mp[:, 3], ms[:, 3], TOK_TILE, dec_seq),
                _mod_tiles(mp[:, 4], ms[:, 4], TOK_TILE, dec_seq),
                _mod_tiles(mp[:, 5], ms[:, 5], CMB_TILE, dec_seq))
        x_all = _moe_layer(x_all, mods, g_pre_ffn[i], g_post_ffn[i], moe_w_router[i], moe_router_bias[i],
                           moe_w_gate[i], moe_w_up[i], moe_w_down[i], sh_w_gate[i], sh_w_up[i], sh_w_down[i],
                           idx_tok, idx_cmb)

    y_prompt = x_all[:n_p].reshape(n_b, seq, D_MODEL)
    y_sample = x_all[n_p:].reshape(n_db, dec_seq, D_MODEL)
    stack = lambda items, c: jnp.stack([it[c] for it in items])
    return (y_prompt, y_sample, stack(ssm_p, 0), stack(ssm_p, 1), stack(ssm_s, 0), stack(ssm_s, 1),
            stack(kv_p, 0), stack(kv_p, 1), stack(kv_s, 0), stack(kv_s, 1))
```
